```python
import math
import jax
import jax.numpy as jnp
from jax import lax
import numpy as np


D_MODEL = 1024
BATCH = 1
SEQ = 16384
DEPTH = 2

MEM_LEN = 256
GROUP_WIDTH = D_MODEL // 2
N_GROUPS = 4
MIX_WIDTH = N_GROUPS * GROUP_WIDTH

SWA_HEADS = 8
SWA_KV_HEADS = 2
SWA_HEAD_DIM = GROUP_WIDTH // SWA_HEADS
WINDOW = 128
REL_BUCKETS = 32
REL_MAX_DIST = 128
GDN_HEADS = 4
GDN_HEAD_DIM = GROUP_WIDTH // GDN_HEADS
GDN_CONV = 4
GDN_CHUNK = 64
MLA_HEADS = 4
MLA_NOPE_DIM = 128
MLA_ROPE_DIM = 64
MLA_V_DIM = GROUP_WIDTH // MLA_HEADS
MLA_Q_RANK = D_MODEL // 4
MLA_KV_RANK = D_MODEL // 8
ROPE_THETA = 10000.0
Q_BLOCK = 128
MEM_HEADS = 4
MEM_HEAD_DIM = GROUP_WIDTH // MEM_HEADS

RMS_EPS = 1e-6
NEG_INF = -1e30

IN_SPLITS = (
    SWA_HEADS * SWA_HEAD_DIM,
    SWA_KV_HEADS * SWA_HEAD_DIM,
    SWA_KV_HEADS * SWA_HEAD_DIM,
    GROUP_WIDTH,
    3 * GROUP_WIDTH,
    GDN_HEADS,
    GDN_HEADS,
    GROUP_WIDTH,
    MLA_Q_RANK,
    MLA_KV_RANK,
    MLA_ROPE_DIM,
    GROUP_WIDTH,
    MEM_HEADS * MEM_HEAD_DIM,
    GROUP_WIDTH,
)
D_IN = sum(IN_SPLITS)

kernel_name = 'hybrid_parallel_heads_block'


def rmsnorm(t, gain):
    tf = t.astype(jnp.float32)
    tf = tf * lax.rsqrt(jnp.mean(tf * tf, axis=-1, keepdims=True) + RMS_EPS)
    return (tf * gain.astype(jnp.float32)).astype(t.dtype)


def l2norm(t):
    return t * lax.rsqrt(jnp.sum(t * t, axis=-1, keepdims=True) + 1e-6)


def split_columns(t, sizes):
    offs = np.cumsum(sizes)[:-1].tolist()
    return jnp.split(t, offs, axis=-1)


def t5_bucket(rel):
    n = jnp.maximum(rel, 0)
    max_exact = REL_BUCKETS // 2
    nf = jnp.maximum(n, 1).astype(jnp.float32)
    large = max_exact + (jnp.log(nf / max_exact) / math.log(REL_MAX_DIST / max_exact)
                         * (REL_BUCKETS - max_exact)).astype(jnp.int32)
    large = jnp.minimum(large, REL_BUCKETS - 1)
    return jnp.where(n < max_exact, n, large)


def banded_rel_bias(rel_bias, positions):
    B, S = positions.shape
    nb = S // WINDOW
    pq = positions.reshape(B, nb, WINDOW)
    pprev = jnp.pad(pq, ((0, 0), (1, 0), (0, 0)))[:, :-1]
    pk = jnp.concatenate([pprev, pq], axis=2)
    bucket = t5_bucket(pq[..., :, None] - pk[..., None, :])
    bias = jnp.take(rel_bias.T.astype(jnp.float32), bucket, axis=1)
    bias = bias.reshape(SWA_KV_HEADS, SWA_HEADS // SWA_KV_HEADS, B, nb, WINDOW, 2 * WINDOW)
    return jnp.moveaxis(bias, 2, 0)


def rope_tables(positions):
    inv_freq = 1.0 / (ROPE_THETA ** (jnp.arange(0, MLA_ROPE_DIM, 2, dtype=jnp.float32) / MLA_ROPE_DIM))
    ang = positions.astype(jnp.float32)[..., None] * inv_freq
    return jnp.cos(ang), jnp.sin(ang)


def apply_rope(t, cos, sin):
    half = t.shape[-1] // 2
    t1, t2 = t[..., :half], t[..., half:]
    return jnp.concatenate([t1 * cos - t2 * sin, t1 * sin + t2 * cos], axis=-1)


def swa_sink_attention(q, k, v, sinks, bias):
    B, S, HQ, Dh = q.shape
    HKV = k.shape[2]
    G = HQ // HKV
    W = WINDOW
    nb = S // W
    qb = q.astype(jnp.float32).reshape(B, nb, W, HKV, G, Dh)

    def with_prev(t):
        tb = t.astype(jnp.float32).reshape(B, nb, W, HKV, Dh)
        prev = jnp.pad(tb, ((0, 0), (1, 0), (0, 0), (0, 0), (0, 0)))[:, :-1]
        return jnp.concatenate([prev, tb], axis=2)

    kb, vb = with_prev(k), with_prev(v)
    s = jnp.einsum('bnqhgd,bnkhd->bhgnqk', qb, kb) * (Dh ** -0.5) + bias
    qi = jnp.arange(W)[:, None]
    kj = jnp.arange(2 * W)[None, :]
    band = (kj > qi) & (kj <= qi + W)
    valid = band[None] & ((jnp.arange(nb) > 0)[:, None, None] | (kj >= W)[None])
    s = jnp.where(valid, s, NEG_INF)
    sink = sinks.astype(jnp.float32).reshape(HKV, G)[None, :, :, None, None]
    m = jnp.maximum(jnp.max(s, axis=-1), sink)
    p = jnp.exp(s - m[..., None])
    denom = jnp.sum(p, axis=-1) + jnp.exp(sink - m)
    p = p / denom[..., None]
    o = jnp.einsum('bhgnqk,bnkhd->bnqhgd', p, vb)
    return o.reshape(B, S, HQ * Dh)


def causal_depthwise_conv(t, w):
    K, C = w.shape
    return lax.conv_general_dilated(
        t.astype(jnp.float32), w.astype(jnp.float32)[:, None, :],
        window_strides=(1,), padding=[(K - 1, 0)],
        dimension_numbers=('NWC', 'WIO', 'NWC'), feature_group_count=C)


def gated_delta_rule_chunked(q, k, v, g, beta):
    B, S, H, Dk = q.shape
    Dv = v.shape[-1]
    C = GDN_CHUNK
    N = S // C

    def chunks(t):
        t = t.reshape(B, N, C, H, *t.shape[3:])
        return jnp.moveaxis(t, 3, 1)

    q = chunks(q * (Dk ** -0.5))
    k = chunks(k)
    v = chunks(v)
    beta = chunks(beta)
    g = jnp.cumsum(chunks(g), axis=-1)
    k_beta = k * beta[..., None]
    v_beta = v * beta[..., None]
    incl = jnp.tril(jnp.ones((C, C), bool))
    strict = jnp.tril(jnp.ones((C, C), bool), -1)
    gdiff = g[..., :, None] - g[..., None, :]
    decay = jnp.where(incl, jnp.exp(jnp.where(incl, gdiff, 0.0)), 0.0)
    L = jnp.where(strict, jnp.einsum('bhncd,bhnmd->bhncm', k_beta, k) * decay, 0.0)
    eye = jnp.eye(C, dtype=jnp.float32)
    T = lax.linalg.triangular_solve(eye + L, jnp.broadcast_to(eye, L.shape),
                                    left_side=True, lower=True)
    u = T @ v_beta
    w = T @ (k_beta * jnp.exp(g)[..., None])
    a_qk = jnp.einsum('bhncd,bhnmd->bhncm', q, k) * decay
    qg = q * jnp.exp(g)[..., None]
    g_last = g[..., -1]
    kd = k * jnp.exp(g_last[..., None] - g)[..., None]

    def step(state, inp):
        qg_i, a_i, u_i, w_i, kd_i, gl_i = inp
        v_new = u_i - jnp.einsum('bhcd,bhde->bhce', w_i, state)
        o_i = jnp.einsum('bhcd,bhde->bhce', qg_i, state) + jnp.einsum('bhcm,bhme->bhce', a_i, v_new)
        state = state * jnp.exp(gl_i)[..., None, None] + jnp.einsum('bhcd,bhce->bhde', kd_i, v_new)
        return state, o_i

    xs = tuple(jnp.moveaxis(t, 2, 0) for t in (qg, a_qk, u, w, kd, g_last))
    state0 = jnp.zeros((B, H, Dk, Dv), jnp.float32)
    _, o = lax.scan(step, state0, xs)
    return jnp.transpose(o, (1, 0, 3, 2, 4)).reshape(B, S, H, Dv)


def mla_causal_attention(q_nope, q_rope, k_nope, k_rope, v):
    B, S, H, _ = q_nope.shape
    nb = S // Q_BLOCK
    scale = (MLA_NOPE_DIM + MLA_ROPE_DIM) ** -0.5
    k_idx = jnp.arange(S)

    def to_blocks(t):
        return jnp.moveaxis(t.reshape(B, nb, Q_BLOCK, *t.shape[2:]), 1, 0)

    def block(args):
        qn, qr, start = args
        s = (jnp.einsum('bqhd,bkhd->bhqk', qn, k_nope)
             + jnp.einsum('bqhd,bkd->bhqk', qr, k_rope)) * scale
        q_idx = start + jnp.arange(Q_BLOCK)
        s = jnp.where(k_idx[None, :] <= q_idx[:, None], s, NEG_INF)
        p = jax.nn.softmax(s, axis=-1)
        return jnp.einsum('bhqk,bkhd->bqhd', p, v)

    o = lax.map(block, (to_blocks(q_nope), to_blocks(q_rope), jnp.arange(nb) * Q_BLOCK))
    return jnp.moveaxis(o, 0, 1).reshape(B, S, H * v.shape[-1])


def memory_cross_attention(q, mk, mv):
    B, S, H, D = q.shape
    s = jnp.einsum('bshd,bmhd->bhsm', q, mk) * (D ** -0.5)
    p = jax.nn.softmax(s, axis=-1)
    return jnp.einsum('bhsm,bmhd->bshd', p, mv).reshape(B, S, H * D)


def hybrid_layer(x, mem, cos, sin, bias_blocks, norm_pre, norm_post, w_in, attn_sinks,
                 gdn_conv_w, gdn_a_log, gdn_dt_bias, gdn_norm, mla_q_norm, mla_kv_norm,
                 mla_w_uq, mla_w_ukv, mem_norm, mem_w_kv, w_out):
    B, S, _ = x.shape
    f32 = jnp.float32
    h = rmsnorm(x, norm_pre)
    (a_q, a_k, a_v, a_z, b_qkv, b_a, b_b, b_z,
     c_cq, c_ckv, c_kr, c_z, d_q, d_z) = split_columns(h @ w_in, IN_SPLITS)

    o_a = swa_sink_attention(a_q.reshape(B, S, SWA_HEADS, SWA_HEAD_DIM),
                             a_k.reshape(B, S, SWA_KV_HEADS, SWA_HEAD_DIM),
                             a_v.reshape(B, S, SWA_KV_HEADS, SWA_HEAD_DIM),
                             attn_sinks, bias_blocks)
    o_a = o_a * jax.nn.silu(a_z.astype(f32))

    qkv = jax.nn.silu(causal_depthwise_conv(b_qkv, gdn_conv_w))
    g_q, g_k, g_v = jnp.split(qkv, 3, axis=-1)
    g_q = l2norm(g_q.reshape(B, S, GDN_HEADS, GDN_HEAD_DIM))
    g_k = l2norm(g_k.reshape(B, S, GDN_HEADS, GDN_HEAD_DIM))
    g_v = g_v.reshape(B, S, GDN_HEADS, GDN_HEAD_DIM)
    beta = jax.nn.sigmoid(b_b.astype(f32))
    log_decay = -jnp.exp(gdn_a_log.astype(f32)) * jax.nn.softplus(b_a.astype(f32) + gdn_dt_bias.astype(f32))
    o_b = gated_delta_rule_chunked(g_q, g_k, g_v, log_decay, beta)
    o_b = rmsnorm(o_b, gdn_norm) * jax.nn.silu(b_z.astype(f32).reshape(B, S, GDN_HEADS, GDN_HEAD_DIM))
    o_b = o_b.reshape(B, S, GROUP_WIDTH)

    cq = rmsnorm(c_cq, mla_q_norm)
    qh = (cq @ mla_w_uq).astype(f32).reshape(B, S, MLA_HEADS, MLA_NOPE_DIM + MLA_ROPE_DIM)
    q_nope, q_rope = qh[..., :MLA_NOPE_DIM], qh[..., MLA_NOPE_DIM:]
    q_rope = apply_rope(q_rope, cos[:, :, None, :], sin[:, :, None, :])
    ckv = rmsnorm(c_ckv, mla_kv_norm)
    kv = (ckv @ mla_w_ukv).astype(f32).reshape(B, S, MLA_HEADS, MLA_NOPE_DIM + MLA_V_DIM)
    k_nope, m_v = kv[..., :MLA_NOPE_DIM], kv[..., MLA_NOPE_DIM:]
    k_rope = apply_rope(c_kr.astype(f32), cos, sin)
    o_c = mla_causal_attention(q_nope, q_rope, k_nope, k_rope, m_v)
    o_c = o_c * jax.nn.silu(c_z.astype(f32))

    mkv = (rmsnorm(mem, mem_norm) @ mem_w_kv).astype(f32)
    mk, mv = jnp.split(mkv, 2, axis=-1)
    M = mem.shape[1]
    o_d = memory_cross_attention(d_q.astype(f32).reshape(B, S, MEM_HEADS, MEM_HEAD_DIM),
                                 mk.reshape(B, M, MEM_HEADS, MEM_HEAD_DIM),
                                 mv.reshape(B, M, MEM_HEADS, MEM_HEAD_DIM))
    o_d = o_d * jax.nn.silu(d_z.astype(f32))

    mixed = jnp.concatenate([o_a, o_b, o_c, o_d], axis=-1).astype(x.dtype)
    y = mixed @ w_out
    return x + rmsnorm(y, norm_post)


def setup_inputs(seed: int = 0) -> dict:
    key = jax.random.key(seed)
    ks = jax.random.split(key, 20)

    def nrm(k, shape, scale):
        return jax.random.normal(k, shape, jnp.float32) * scale

    def gain(k, shape):
        return 1.0 + 0.02 * jax.random.normal(k, shape, jnp.float32)

    x = nrm(ks[0], (BATCH, SEQ, D_MODEL), 1.0)
    mem = nrm(ks[1], (BATCH, MEM_LEN, D_MODEL), 1.0)
    offset = jax.random.randint(ks[2], (BATCH, 1), 0, 1024, dtype=jnp.int32)
    positions = offset + jnp.arange(SEQ, dtype=jnp.int32)[None, :]
    rel_bias = nrm(ks[3], (REL_BUCKETS, SWA_HEADS), 0.5)
    norm_pre = gain(ks[4], (DEPTH, D_MODEL))
    norm_post = gain(ks[5], (DEPTH, D_MODEL))
    w_in = nrm(ks[6], (DEPTH, D_MODEL, D_IN), D_MODEL ** -0.5)
    attn_sinks = nrm(ks[7], (DEPTH, SWA_HEADS), 0.5)
    gdn_conv_w = nrm(ks[8], (DEPTH, GDN_CONV, 3 * GROUP_WIDTH), GDN_CONV ** -0.5)
    gdn_a_log = jnp.log(jax.random.uniform(ks[9], (DEPTH, GDN_HEADS), jnp.float32, 1.0, 16.0))
    dt = jnp.exp(jax.random.uniform(ks[10], (DEPTH, GDN_HEADS), jnp.float32,
                                    math.log(1e-3), math.log(1e-1)))
    gdn_dt_bias = jnp.log(jnp.expm1(dt))
    gdn_norm = gain(ks[11], (DEPTH, GDN_HEAD_DIM))
    mla_q_norm = gain(ks[12], (DEPTH, MLA_Q_RANK))
    mla_kv_norm = gain(ks[13], (DEPTH, MLA_KV_RANK))
    mla_w_uq = nrm(ks[14], (DEPTH, MLA_Q_RANK, MLA_HEADS * (MLA_NOPE_DIM + MLA_ROPE_DIM)), MLA_Q_RANK ** -0.5)
    mla_w_ukv = nrm(ks[15], (DEPTH, MLA_KV_RANK, MLA_HEADS * (MLA_NOPE_DIM + MLA_V_DIM)), MLA_KV_RANK ** -0.5)
    mem_norm = gain(ks[16], (DEPTH, D_MODEL))
    mem_w_kv = nrm(ks[17], (DEPTH, D_MODEL, 2 * GROUP_WIDTH), D_MODEL ** -0.5)
    w_out = nrm(ks[18], (DEPTH, MIX_WIDTH, D_MODEL), MIX_WIDTH ** -0.5)
    return {'x': x, 'mem': mem, 'positions': positions, 'rel_bias': rel_bias,
            'norm_pre': norm_pre, 'norm_post': norm_post, 'w_in': w_in,
            'attn_sinks': attn_sinks, 'gdn_conv_w': gdn_conv_w, 'gdn_a_log': gdn_a_log,
            'gdn_dt_bias': gdn_dt_bias, 'gdn_norm': gdn_norm, 'mla_q_norm': mla_q_norm,
            'mla_kv_norm': mla_kv_norm, 'mla_w_uq': mla_w_uq, 'mla_w_ukv': mla_w_ukv,
            'mem_norm': mem_norm, 'mem_w_kv': mem_w_kv, 'w_out': w_out}


def reference(x, mem, positions, rel_bias, norm_pre, norm_post, w_in, attn_sinks,
              gdn_conv_w, gdn_a_log, gdn_dt_bias, gdn_norm, mla_q_norm, mla_kv_norm,
              mla_w_uq, mla_w_ukv, mem_norm, mem_w_kv, w_out):
    bias_blocks = banded_rel_bias(rel_bias, positions)
    cos, sin = rope_tables(positions)
    for l in range(DEPTH):
        x = hybrid_layer(x, mem, cos, sin, bias_blocks, norm_pre[l], norm_post[l], w_in[l],
                         attn_sinks[l], gdn_conv_w[l], gdn_a_log[l], gdn_dt_bias[l], gdn_norm[l],
                         mla_q_norm[l], mla_kv_norm[l], mla_w_uq[l], mla_w_ukv[l],
                         mem_norm[l], mem_w_kv[l], w_out[l])
    return x
```

```python
import functools
import math

import jax
import jax.numpy as jnp
from jax import lax
from jax.experimental import pallas as pl
from jax.experimental.pallas import tpu as pltpu

F32 = jnp.float32
BF16 = jnp.bfloat16
HIGHEST = lax.Precision.HIGHEST

D_MODEL = 1024
GROUP_WIDTH = 512
SWA_HEADS = 8
SWA_KV_HEADS = 2
SWA_HEAD_DIM = 64
WINDOW = 128
REL_BUCKETS = 32
REL_MAX_DIST = 128
GDN_HEADS = 4
GDN_HEAD_DIM = 128
GDN_CONV = 4
GDN_CHUNK = 64
MLA_HEADS = 4
MLA_NOPE_DIM = 128
MLA_ROPE_DIM = 64
MLA_V_DIM = 128
MLA_Q_RANK = 256
MLA_KV_RANK = 128
ROPE_THETA = 10000.0
MEM_HEADS = 4
MEM_HEAD_DIM = 128
RMS_EPS = 1e-6
NEG_INF = -1e30

YA_W = 1280
YB_W = 2048
YC_W = 1152
YD_W = 1024
YC_GATE = 640
MLA_QK_PAD = 256

VMEM_LIMIT = 56 * 1024 * 1024


def _cparams(sem):
    return pltpu.CompilerParams(dimension_semantics=sem, vmem_limit_bytes=VMEM_LIMIT)


def _nt(a, b, precision=None):
    return lax.dot_general(a, b, (((1,), (1,)), ((), ())),
                           preferred_element_type=F32, precision=precision)


def _mm(a, b, precision=None):
    return jnp.dot(a, b, preferred_element_type=F32, precision=precision)


def _silu(z):
    return z * jax.nn.sigmoid(z)


def _rms(t, gain):
    return t * lax.rsqrt(jnp.mean(t * t, axis=-1, keepdims=True) + RMS_EPS) * gain


def _inproj_body(x_ref, g_ref, wa_ref, wb_ref, wc_ref, wd_ref, wab_ref, wabt_ref,
                 ya_ref, yb_ref, yc_ref, yd_ref, ab_ref, abt_ref):
    h = _rms(x_ref[...], g_ref[...])
    hb = h.astype(BF16)
    for w_ref, y_ref in ((wa_ref, ya_ref), (wb_ref, yb_ref), (wc_ref, yc_ref), (wd_ref, yd_ref)):
        n = w_ref.shape[1]
        for c0 in range(0, n, 512):
            c1 = min(c0 + 512, n)
            y_ref[:, c0:c1] = _mm(hb, w_ref[:, c0:c1]).astype(y_ref.dtype)
    ab_ref[...] = _mm(h, wab_ref[...], precision=HIGHEST)
    abt_ref[...] = _nt(wabt_ref[...], h, precision=HIGHEST)


def _inproj(x, gain, wa, wb, wc, wd, wab, wabt, tm):
    s = x.shape[0]
    full = lambda a: pl.BlockSpec(a.shape, lambda i: (0, 0))
    row = lambda w: pl.BlockSpec((tm, w), lambda i: (i, 0))
    return pl.pallas_call(
        _inproj_body,
        grid=(s // tm,),
        in_specs=[row(D_MODEL), full(gain), full(wa), full(wb), full(wc), full(wd), full(wab), full(wabt)],
        out_specs=[row(YA_W), row(YB_W), row(YC_W), row(YD_W), row(128),
                   pl.BlockSpec((8, tm), lambda i: (0, i))],
        out_shape=[jax.ShapeDtypeStruct((s, YA_W), BF16), jax.ShapeDtypeStruct((s, YB_W), BF16),
                   jax.ShapeDtypeStruct((s, YC_W), BF16), jax.ShapeDtypeStruct((s, YD_W), BF16),
                   jax.ShapeDtypeStruct((s, 128), F32), jax.ShapeDtypeStruct((8, s), F32)],
        compiler_params=_cparams(("parallel",)),
        name="inproj",
    )(x, gain, wa, wb, wc, wd, wab, wabt)


def _outproj_body(oa_ref, ob_ref, oc_ref, od_ref, w_ref, x_ref, g_ref, out_ref):
    y = _mm(oa_ref[...], w_ref[0:512, :])
    y = y + _mm(ob_ref[...], w_ref[512:1024, :])
    y = y + _mm(oc_ref[...], w_ref[1024:1536, :])
    y = y + _mm(od_ref[...], w_ref[1536:2048, :])
    out_ref[...] = x_ref[...] + _rms(y, g_ref[...])


def _outproj(oa, ob, oc, od, w, x, gain, tm):
    s = x.shape[0]
    full = lambda a: pl.BlockSpec(a.shape, lambda i: (0, 0))
    row = lambda wd: pl.BlockSpec((tm, wd), lambda i: (i, 0))
    return pl.pallas_call(
        _outproj_body,
        grid=(s // tm,),
        in_specs=[row(512), row(512), row(512), row(512), full(w), row(D_MODEL), full(gain)],
        out_specs=row(D_MODEL),
        out_shape=jax.ShapeDtypeStruct((s, D_MODEL), F32),
        compiler_params=_cparams(("parallel",)),
        name="outproj",
    )(oa, ob, oc, od, w, x, gain)


def _swa_body(sink_ref, cur_ref, prev_ref, bias_ref, o_ref, *, nsub):
    i = pl.program_id(0)
    w = WINDOW
    scale = SWA_HEAD_DIM ** -0.5
    group = SWA_HEADS // SWA_KV_HEADS
    col = lax.broadcasted_iota(jnp.int32, (w, 2 * w), 1)
    for b in range(nsub):
        r0 = b * w
        if b == 0:
            kp, vp = prev_ref[:, 0:128], prev_ref[:, 128:256]
        else:
            kp, vp = cur_ref[r0 - w:r0, 512:640], cur_ref[r0 - w:r0, 640:768]
        kcat = jnp.concatenate([kp, cur_ref[r0:r0 + w, 512:640]], axis=0)
        vcat = jnp.concatenate([vp, cur_ref[r0:r0 + w, 640:768]], axis=0)
        for h in range(SWA_HEADS):
            hk = h // group
            q = cur_ref[r0:r0 + w, h * 64:(h + 1) * 64]
            s = _nt(q, kcat[:, hk * 64:(hk + 1) * 64]) * scale + bias_ref[h]
            if b == 0:
                s = jnp.where(jnp.logical_and(i == 0, col < w), NEG_INF, s)
            sink = sink_ref[h]
            m = jnp.maximum(jnp.max(s, axis=-1, keepdims=True), sink)
            p = jnp.exp(s - m)
            denom = jnp.sum(p, axis=-1, keepdims=True) + jnp.exp(sink - m)
            o = _mm(p.astype(BF16), vcat[:, hk * 64:(hk + 1) * 64]) / denom
            z = cur_ref[r0:r0 + w, 768 + h * 64:768 + (h + 1) * 64].astype(F32)
            o_ref[r0:r0 + w, h * 64:(h + 1) * 64] = (o * _silu(z)).astype(o_ref.dtype)


def _swa(ya, sinks, bias_tab, tq):
    s = ya.shape[0]
    nsub = tq // WINDOW
    return pl.pallas_call(
        functools.partial(_swa_body, nsub=nsub),
        grid=(s // tq,),
        in_specs=[pl.BlockSpec(memory_space=pltpu.SMEM),
                  pl.BlockSpec((tq, YA_W), lambda i: (i, 0)),
                  pl.BlockSpec((WINDOW, 256), lambda i: (jnp.maximum(i * nsub - 1, 0), 2)),
                  pl.BlockSpec(bias_tab.shape, lambda i: (0, 0, 0))],
        out_specs=pl.BlockSpec((tq, GROUP_WIDTH), lambda i: (i, 0)),
        out_shape=jax.ShapeDtypeStruct((s, GROUP_WIDTH), BF16),
        compiler_params=_cparams(("parallel",)),
        name="swa",
    )(sinks, ya, ya, bias_tab)


def _softplus(t):
    return jnp.maximum(t, 0.0) + jnp.log1p(jnp.exp(-jnp.abs(t)))


def _gdn_body(yb_ref, halo_ref, ab_ref, abt_ref, cw_ref, alc_ref, dtc_ref, alr_ref, dtr_ref,
              gn_ref, o_ref, st_ref, *, nch):
    i = pl.program_id(0)
    c64 = GDN_CHUNK
    tb = nch * c64
    hd = GDN_HEAD_DIM

    @pl.when(i == 0)
    def _():
        st_ref[...] = jnp.zeros_like(st_ref)

    x = yb_ref[:, 0:3 * GROUP_WIDTH].astype(F32)
    halo = halo_ref[:, 0:3 * GROUP_WIDTH].astype(F32)
    halo = jnp.where(i > 0, halo, 0.0)
    xh = jnp.concatenate([halo, x], axis=0)
    base = 16 - (GDN_CONV - 1)
    conv = cw_ref[0:1, :] * xh[base:base + tb]
    for j in range(1, GDN_CONV):
        conv = conv + cw_ref[j:j + 1, :] * xh[base + j:base + j + tb]
    qkv = _silu(conv)

    ab = ab_ref[...]
    ld_col = -jnp.exp(alc_ref[...]) * _softplus(ab + dtc_ref[...])
    beta_col = jax.nn.sigmoid(ab)
    ld_row = -jnp.exp(alr_ref[...]) * _softplus(abt_ref[...] + dtr_ref[...])

    ri = lax.broadcasted_iota(jnp.int32, (tb, tb), 0)
    ci = lax.broadcasted_iota(jnp.int32, (tb, tb), 1)
    same = (ri // c64) == (ci // c64)
    tri_l = jnp.where(jnp.logical_and(same, ci <= ri), 1.0, 0.0).astype(F32)
    tri_u = jnp.where(jnp.logical_and(same, ri <= ci), 1.0, 0.0).astype(F32)
    g_col = _mm(tri_l, ld_col, precision=HIGHEST)
    g_row = _mm(ld_row, tri_u, precision=HIGHEST)

    r64 = lax.broadcasted_iota(jnp.int32, (c64, c64), 0)
    c64i = lax.broadcasted_iota(jnp.int32, (c64, c64), 1)
    incl = c64i <= r64
    strict = c64i < r64

    for h in range(GDN_HEADS):
        qh = qkv[:, h * hd:(h + 1) * hd]
        kh = qkv[:, GROUP_WIDTH + h * hd:GROUP_WIDTH + (h + 1) * hd]
        vh = qkv[:, 2 * GROUP_WIDTH + h * hd:2 * GROUP_WIDTH + (h + 1) * hd]
        qh = qh * lax.rsqrt(jnp.sum(qh * qh, axis=-1, keepdims=True) + 1e-6) * (hd ** -0.5)
        kh = kh * lax.rsqrt(jnp.sum(kh * kh, axis=-1, keepdims=True) + 1e-6)
        state = st_ref[h]
        for c in range(nch):
            r0, r1 = c * c64, (c + 1) * c64
            q, k, v = qh[r0:r1], kh[r0:r1], vh[r0:r1]
            gc = g_col[r0:r1, h:h + 1]
            gr = g_row[h:h + 1, r0:r1]
            gl = g_col[r1 - 1:r1, h:h + 1]
            beta = beta_col[r0:r1, GDN_HEADS + h:GDN_HEADS + h + 1]
            eg = jnp.exp(gc)
            decay = jnp.where(incl, jnp.exp(jnp.where(incl, gc - gr, 0.0)), 0.0)
            kb = k * beta
            kbf = k.astype(BF16)
            low = jnp.where(strict, _nt(kb.astype(BF16), kbf) * decay, 0.0)
            mp = -low
            r = mp
            for _ in range(5):
                mpb = mp.astype(BF16)
                mp = _mm(mpb, mpb)
                r = r + mp + _mm(r.astype(BF16), mp.astype(BF16))
            rhs = jnp.concatenate([v * beta, kb * eg], axis=1)
            uw = rhs + _mm(r.astype(BF16), rhs.astype(BF16))
            u, wm = uw[:, 0:hd], uw[:, hd:2 * hd]
            aqk = _nt(q.astype(BF16), kbf) * decay
            qg = q * eg
            kd = k * jnp.exp(gl - gc)
            sb = state.astype(BF16)
            v_new = u - _mm(wm.astype(BF16), sb)
            vnb = v_new.astype(BF16)
            o = _mm(qg.astype(BF16), sb) + _mm(aqk.astype(BF16), vnb)
            state = state * jnp.exp(gl) + _mm(kd.T.astype(BF16), vnb)
            z = yb_ref[r0:r1, 3 * GROUP_WIDTH + h * hd:3 * GROUP_WIDTH + (h + 1) * hd].astype(F32)
            o_ref[r0:r1, h * hd:(h + 1) * hd] = (_rms(o, gn_ref[...]) * _silu(z)).astype(o_ref.dtype)
        st_ref[h] = state


def _gdn(yb, ab, abt, conv_w, alc, dtc, alr, dtr, gnorm, nch):
    s = yb.shape[0]
    tb = nch * GDN_CHUNK
    full = lambda a: pl.BlockSpec(a.shape, lambda i: (0,) * a.ndim)
    return pl.pallas_call(
        functools.partial(_gdn_body, nch=nch),
        grid=(s // tb,),
        in_specs=[pl.BlockSpec((tb, YB_W), lambda i: (i, 0)),
                  pl.BlockSpec((16, YB_W), lambda i: (jnp.maximum(i * (tb // 16) - 1, 0), 0)),
                  pl.BlockSpec((tb, 128), lambda i: (i, 0)),
                  pl.BlockSpec((8, tb), lambda i: (0, i)),
                  full(conv_w), full(alc), full(dtc), full(alr), full(dtr), full(gnorm)],
        out_specs=pl.BlockSpec((tb, GROUP_WIDTH), lambda i: (i, 0)),
        out_shape=jax.ShapeDtypeStruct((s, GROUP_WIDTH), BF16),
        scratch_shapes=[pltpu.VMEM((GDN_HEADS, GDN_HEAD_DIM, GDN_HEAD_DIM), F32)],
        compiler_params=_cparams(("arbitrary",)),
        name="gdn",
    )(yb, yb, ab, abt, conv_w, alc, dtc, alr, dtr, gnorm)


def _mla_proj_body(yc_ref, ct_ref, st_ref, qn_ref, kvn_ref, wqn_ref, wqr_ref, wqs_ref, wkn_ref, wv_ref,
                   q_ref, k_ref, v_ref):
    scale = (MLA_NOPE_DIM + MLA_ROPE_DIM) ** -0.5
    ct, st = ct_ref[...], st_ref[...]
    cqn = _rms(yc_ref[:, 0:256].astype(F32), qn_ref[...]).astype(BF16)
    qn = _mm(cqn, wqn_ref[...])
    qr = _mm(cqn, wqr_ref[...])
    qs = _mm(cqn, wqs_ref[...])
    ckvn = _rms(yc_ref[:, 256:384].astype(F32), kvn_ref[...]).astype(BF16)
    kn = _mm(ckvn, wkn_ref[...])
    vv = _mm(ckvn, wv_ref[...])
    kr = (yc_ref[:, 384:512].astype(F32) * ct + yc_ref[:, 512:640].astype(F32) * st).astype(BF16)
    for h in range(MLA_HEADS):
        sl = slice(h * 128, (h + 1) * 128)
        q_ref[h, :, 0:128] = (qn[:, sl] * scale).astype(BF16)
        q_ref[h, :, 128:256] = ((qr[:, sl] * ct + qs[:, sl] * st) * scale).astype(BF16)
        k_ref[h, :, 0:128] = kn[:, sl].astype(BF16)
        k_ref[h, :, 128:256] = kr
        v_ref[h] = vv[:, sl].astype(BF16)


def _mla_proj(yc, ct, st, qnorm, kvnorm, wqn, wqr, wqs, wkn, wv, tm):
    s = yc.shape[0]
    full = lambda a: pl.BlockSpec(a.shape, lambda i: (0, 0))
    return pl.pallas_call(
        _mla_proj_body,
        grid=(s // tm,),
        in_specs=[pl.BlockSpec((tm, 640), lambda i: (i, 0)),
                  pl.BlockSpec((tm, 128), lambda i: (i, 0)), pl.BlockSpec((tm, 128), lambda i: (i, 0)),
                  full(qnorm), full(kvnorm), full(wqn), full(wqr), full(wqs), full(wkn), full(wv)],
        out_specs=[pl.BlockSpec((MLA_HEADS, tm, MLA_QK_PAD), lambda i: (0, i, 0)),
                   pl.BlockSpec((MLA_HEADS, tm, MLA_QK_PAD), lambda i: (0, i, 0)),
                   pl.BlockSpec((MLA_HEADS, tm, MLA_V_DIM), lambda i: (0, i, 0))],
        out_shape=[jax.ShapeDtypeStruct((MLA_HEADS, s, MLA_QK_PAD), BF16),
                   jax.ShapeDtypeStruct((MLA_HEADS, s, MLA_QK_PAD), BF16),
                   jax.ShapeDtypeStruct((MLA_HEADS, s, MLA_V_DIM), BF16)],
        compiler_params=_cparams(("parallel",)),
        name="mla_proj",
    )(yc, ct, st, qnorm, kvnorm, wqn, wqr, wqs, wkn, wv)


def _flash_body(q_ref, k_ref, v_ref, z_ref, o_ref, m_sc, l_sc, acc_sc, *, tq, tk):
    qi = pl.program_id(1)
    ki = pl.program_id(2)
    last = (qi * tq + tq - 1) // tk

    @pl.when(ki == 0)
    def _():
        m_sc[...] = jnp.full_like(m_sc, NEG_INF)
        l_sc[...] = jnp.zeros_like(l_sc)
        acc_sc[...] = jnp.zeros_like(acc_sc)

    def update(masked):
        s = _nt(q_ref[0], k_ref[0])
        if masked:
            rows = qi * tq + lax.broadcasted_iota(jnp.int32, (tq, tk), 0)
            cols = ki * tk + lax.broadcasted_iota(jnp.int32, (tq, tk), 1)
            s = jnp.where(cols <= rows, s, NEG_INF)
        m_prev = m_sc[...]
        m_new = jnp.maximum(m_prev, jnp.max(s, axis=-1, keepdims=True))
        alpha = jnp.exp(m_prev - m_new)
        p = jnp.exp(s - m_new)
        l_sc[...] = alpha * l_sc[...] + jnp.sum(p, axis=-1, keepdims=True)
        acc_sc[...] = alpha * acc_sc[...] + _mm(p.astype(BF16), v_ref[0])
        m_sc[...] = m_new

    crosses = ki * tk + tk - 1 > qi * tq

    @pl.when(jnp.logical_and(ki <= last, crosses))
    def _():
        update(True)

    @pl.when(jnp.logical_and(ki <= last, jnp.logical_not(crosses)))
    def _():
        update(False)

    @pl.when(ki == last)
    def _():
        z = z_ref[...].astype(F32)
        o_ref[...] = (acc_sc[...] / l_sc[...] * _silu(z)).astype(o_ref.dtype)


def _flash(q, k, v, yc, tq, tk):
    nh, s, _ = q.shape
    gate_blk = YC_GATE // 128

    def kv_map(h, qi, ki):
        return (h, jnp.minimum(ki, (qi * tq + tq - 1) // tk), 0)

    return pl.pallas_call(
        functools.partial(_flash_body, tq=tq, tk=tk),
        grid=(nh, s // tq, s // tk),
        in_specs=[pl.BlockSpec((1, tq, MLA_QK_PAD), lambda h, qi, ki: (h, qi, 0)),
                  pl.BlockSpec((1, tk, MLA_QK_PAD), kv_map),
                  pl.BlockSpec((1, tk, MLA_V_DIM), kv_map),
                  pl.BlockSpec((tq, 128), lambda h, qi, ki: (qi, gate_blk + h))],
        out_specs=pl.BlockSpec((tq, 128), lambda h, qi, ki: (qi, h)),
        out_shape=jax.ShapeDtypeStruct((s, GROUP_WIDTH), BF16),
        scratch_shapes=[pltpu.VMEM((tq, 1), F32), pltpu.VMEM((tq, 1), F32), pltpu.VMEM((tq, MLA_V_DIM), F32)],
        compiler_params=_cparams(("parallel", "parallel", "arbitrary")),
        name="mla_flash",
    )(q, k, v, yc)


def _memkv_body(mem_ref, g_ref, w_ref, o_ref):
    hb = _rms(mem_ref[...], g_ref[...]).astype(BF16)
    o_ref[...] = _mm(hb, w_ref[...]).astype(o_ref.dtype)


def _memkv(mem, gain, w):
    m = mem.shape[0]
    return pl.pallas_call(
        _memkv_body,
        out_shape=jax.ShapeDtypeStruct((m, 2 * GROUP_WIDTH), BF16),
        compiler_params=pltpu.CompilerParams(vmem_limit_bytes=VMEM_LIMIT),
        name="memkv",
    )(mem, gain, w)


def _memattn_body(yd_ref, mkv_ref, o_ref):
    scale = MEM_HEAD_DIM ** -0.5
    for h in range(MEM_HEADS):
        sl = slice(h * 128, (h + 1) * 128)
        s = _nt(yd_ref[:, sl], mkv_ref[:, sl]) * scale
        m = jnp.max(s, axis=-1, keepdims=True)
        p = jnp.exp(s - m)
        denom = jnp.sum(p, axis=-1, keepdims=True)
        o = _mm(p.astype(BF16), mkv_ref[:, GROUP_WIDTH + h * 128:GROUP_WIDTH + (h + 1) * 128]) / denom
        z = yd_ref[:, GROUP_WIDTH + h * 128:GROUP_WIDTH + (h + 1) * 128].astype(F32)
        o_ref[:, sl] = (o * _silu(z)).astype(o_ref.dtype)


def _memattn(yd, mkv, tm):
    s = yd.shape[0]
    return pl.pallas_call(
        _memattn_body,
        grid=(s // tm,),
        in_specs=[pl.BlockSpec((tm, YD_W), lambda i: (i, 0)),
                  pl.BlockSpec(mkv.shape, lambda i: (0, 0))],
        out_specs=pl.BlockSpec((tm, GROUP_WIDTH), lambda i: (i, 0)),
        out_shape=jax.ShapeDtypeStruct((s, GROUP_WIDTH), BF16),
        compiler_params=_cparams(("parallel",)),
        name="memattn",
    )(yd, mkv)


def _t5_bucket(rel):
    n = jnp.maximum(rel, 0)
    max_exact = REL_BUCKETS // 2
    nf = jnp.maximum(n, 1).astype(F32)
    large = max_exact + (jnp.log(nf / max_exact) / math.log(REL_MAX_DIST / max_exact)
                         * (REL_BUCKETS - max_exact)).astype(jnp.int32)
    large = jnp.minimum(large, REL_BUCKETS - 1)
    return jnp.where(n < max_exact, n, large)


def _band_bias_table(rel_bias):
    qi = jnp.arange(WINDOW, dtype=jnp.int32)[:, None]
    kj = jnp.arange(2 * WINDOW, dtype=jnp.int32)[None, :]
    rel = qi + WINDOW - kj
    valid = jnp.logical_and(rel >= 0, rel < WINDOW)
    bias = jnp.take(rel_bias.T.astype(F32), _t5_bucket(rel), axis=1)
    return jnp.where(valid[None], bias, NEG_INF)


def _pad_cols(t, width):
    return jnp.pad(t, ((0, 0), (0, width - t.shape[1])))


def _split_w_in(w):
    sizes = (512, 128, 128, 512, 1536, 4, 4, 512, 256, 128, 64, 512, 512, 512)
    offs = [0]
    for n in sizes:
        offs.append(offs[-1] + n)
    return [w[:, offs[j]:offs[j + 1]] for j in range(len(sizes))]


def _swap_halves(t):
    half = t.shape[-1] // 2
    return jnp.concatenate([t[..., half:], t[..., :half]], axis=-1)


def _prep_layer(w_in, mla_w_uq, mla_w_ukv):
    (a_q, a_k, a_v, a_z, b_qkv, b_a, b_b, b_z, c_cq, c_ckv, c_kr, c_z, d_q, d_z) = _split_w_in(w_in)
    wa = jnp.concatenate([a_q, a_k, a_v, a_z], axis=1).astype(BF16)
    wb = jnp.concatenate([b_qkv, b_z], axis=1).astype(BF16)
    wc = jnp.concatenate([c_cq, c_ckv, _pad_cols(c_kr, 128), _pad_cols(_swap_halves(c_kr), 128), c_z],
                         axis=1).astype(BF16)
    wd = jnp.concatenate([d_q, d_z], axis=1).astype(BF16)
    wab8 = jnp.concatenate([b_a, b_b], axis=1)
    wab = _pad_cols(wab8, 128)
    wabt = wab8.T
    uq = mla_w_uq.reshape(MLA_Q_RANK, MLA_HEADS, MLA_NOPE_DIM + MLA_ROPE_DIM)
    uq_n, uq_r = uq[..., :MLA_NOPE_DIM], uq[..., MLA_NOPE_DIM:]
    pad_r = lambda t: jnp.pad(t, ((0, 0), (0, 0), (0, 128 - MLA_ROPE_DIM))).reshape(MLA_Q_RANK, MLA_HEADS * 128)
    wqn = uq_n.reshape(MLA_Q_RANK, MLA_HEADS * MLA_NOPE_DIM).astype(BF16)
    wqr = pad_r(uq_r).astype(BF16)
    wqs = pad_r(_swap_halves(uq_r)).astype(BF16)
    ukv = mla_w_ukv.reshape(MLA_KV_RANK, MLA_HEADS, MLA_NOPE_DIM + MLA_V_DIM)
    wkn = ukv[..., :MLA_NOPE_DIM].reshape(MLA_KV_RANK, MLA_HEADS * MLA_NOPE_DIM).astype(BF16)
    wv = ukv[..., MLA_NOPE_DIM:].reshape(MLA_KV_RANK, MLA_HEADS * MLA_V_DIM).astype(BF16)
    return wa, wb, wc, wd, wab, wabt, wqn, wqr, wqs, wkn, wv


def _rope_tables(positions):
    inv_freq = 1.0 / (ROPE_THETA ** (jnp.arange(0, MLA_ROPE_DIM, 2, dtype=F32) / MLA_ROPE_DIM))
    ang = positions.astype(F32)[:, None] * inv_freq
    cos, sin = jnp.cos(ang), jnp.sin(ang)
    ct = _pad_cols(jnp.concatenate([cos, cos], axis=1), 128)
    st = _pad_cols(jnp.concatenate([-sin, sin], axis=1), 128)
    return ct, st


def _row(v, width=None):
    v = v.reshape(1, -1).astype(F32)
    return v if width is None else _pad_cols(v, width)


def _col8(v):
    return jnp.pad(v.reshape(-1, 1).astype(F32), ((0, 8 - v.shape[0]), (0, 0)))


TM_PROJ = 256
TQ_SWA = 256
GDN_CHUNKS_PER_STEP = 2
TM_MLA_PROJ = 512
TQ_FLASH = 512
TK_FLASH = 512
TM_MEM = 256


def kernel(x, mem, positions, rel_bias, norm_pre, norm_post, w_in, attn_sinks, gdn_conv_w, gdn_a_log,
           gdn_dt_bias, gdn_norm, mla_q_norm, mla_kv_norm, mla_w_uq, mla_w_ukv, mem_norm, mem_w_kv, w_out):
    assert x.shape[0] == 1 and mem.shape[0] == 1
    depth = w_in.shape[0]
    xs = x[0]
    mem2 = mem[0]
    s = xs.shape[0]
    assert s % max(TM_PROJ, TQ_SWA, TM_MLA_PROJ, TQ_FLASH, TK_FLASH, TM_MEM) == 0
    bias_tab = _band_bias_table(rel_bias)
    ct, st = _rope_tables(positions[0])
    for l in range(depth):
        wa, wb, wc, wd, wab, wabt, wqn, wqr, wqs, wkn, wv = _prep_layer(w_in[l], mla_w_uq[l], mla_w_ukv[l])
        ya, yb, yc, yd, ab, abt = _inproj(xs, _row(norm_pre[l]), wa, wb, wc, wd, wab, wabt, TM_PROJ)
        o_a = _swa(ya, attn_sinks[l].astype(F32), bias_tab, TQ_SWA)
        o_b = _gdn(yb, ab, abt, gdn_conv_w[l].astype(F32), _row(gdn_a_log[l], 128), _row(gdn_dt_bias[l], 128),
                   _col8(gdn_a_log[l]), _col8(gdn_dt_bias[l]), _row(gdn_norm[l]), GDN_CHUNKS_PER_STEP)
        q, k, v = _mla_proj(yc, ct, st, _row(mla_q_norm[l]), _row(mla_kv_norm[l]), wqn, wqr, wqs, wkn, wv,
                            TM_MLA_PROJ)
        o_c = _flash(q, k, v, yc, TQ_FLASH, TK_FLASH)
        mkv = _memkv(mem2, _row(mem_norm[l]), mem_w_kv[l].astype(BF16))
        o_d = _memattn(yd, mkv, TM_MEM)
        xs = _outproj(o_a, o_b, o_c, o_d, w_out[l].astype(BF16), xs, _row(norm_post[l]), TM_PROJ)
    return xs[None]
```
